```python
import jax, jax.numpy as jnp
from jax import lax
import numpy as np

D_MODEL = 1024
BATCH = 8
SEQ = 4096
DEPTH = 1
DEC_BATCH = 8
DEC_SEQ = 8192
PAST_LEN = 128

D_POOL = D_MODEL // 2
D_RWKV = D_MODEL - D_POOL
POOL_WINDOWS = (2, 4, 8, 16)
N_POOL_GROUPS = len(POOL_WINDOWS)
POOL_GROUP = D_POOL // N_POOL_GROUPS
HEAD_DIM = 64
N_HEADS = D_RWKV // HEAD_DIM
R_DECAY = 64
R_AAA = 64
R_GATE = 160
N_DIR = 2
D_FF = 2816
P_RWKV = 3 * D_RWKV + N_DIR * R_DECAY + N_DIR * R_AAA + R_GATE
P_IN = D_POOL + P_RWKV
N_MOD = 9
RMS_EPS = 1e-6
GN_EPS = 64e-5
L2_EPS = 1e-12

kernel_name = "hybrid_pool_rwkv7_macaron_encoder"


def rmsnorm(x, g):
    x32 = x.astype(jnp.float32)
    y = x32 * lax.rsqrt(jnp.mean(x32 * x32, axis=-1, keepdims=True) + RMS_EPS) * g.astype(jnp.float32)
    return y.astype(x.dtype)


def modulate(h, shift, scale):
    return h * (1 + scale) + shift


def swiglu(h, w1, w3, w2):
    return (jax.nn.silu(h @ w1) * (h @ w3)) @ w2


def token_shift(z, mu):
    prev = jnp.pad(z[:, :-1], ((0, 0), (1, 0), (0, 0)))
    nxt = jnp.pad(z[:, 1:], ((0, 0), (0, 1), (0, 0)))
    return z + (0.5 * (prev + nxt) - z) * mu


def pool_mixer(z, pool_w, pool_scale):
    B, T, _ = z.shape
    z = z.astype(jnp.float32)
    cs = jnp.concatenate([jnp.zeros((B, 1, D_POOL), jnp.float32), jnp.cumsum(z, axis=1)], axis=1)
    t = jnp.arange(T)
    outs = []
    for gi, win in enumerate(POOL_WINDOWS):
        sl = slice(gi * POOL_GROUP, (gi + 1) * POOL_GROUP)
        lo = jnp.clip(t - win // 2, 0, T)
        hi = jnp.clip(t + win // 2, 0, T)
        csg = cs[..., sl]
        wsum = jnp.take(csg, hi, axis=1) - jnp.take(csg, lo, axis=1)
        count = (hi - lo).astype(jnp.float32)[None, :, None]
        outs.append(wsum / count - z[..., sl])
    p = jnp.stack(outs, axis=2)
    p = jnp.einsum('btgc,gcd->btgd', p, pool_w.astype(jnp.float32))
    return p.reshape(B, T, D_POOL) * pool_scale.astype(jnp.float32)


def wkv_scan(r, decay, k, v, kk, a, reverse):
    B, T, H, N = r.shape
    xs = tuple(jnp.moveaxis(t_, 1, 0) for t_ in (r, decay, k, v, -kk, kk * a))

    def step(S, inp):
        r_t, w_t, k_t, v_t, na_t, b_t = inp
        sa = jnp.einsum('bhvk,bhk->bhv', S, na_t)
        S = S * w_t[:, :, None, :] + sa[..., None] * b_t[:, :, None, :] + v_t[..., None] * k_t[:, :, None, :]
        y = jnp.einsum('bhvk,bhk->bhv', S, r_t)
        return S, y

    S0 = jnp.zeros((B, H, N, N), jnp.float32)
    _, y = lax.scan(step, S0, xs, reverse=reverse)
    return jnp.moveaxis(y, 0, 1)


def rwkv7_mixer(z, w0, w2, a0, a2, g2, k_k, k_a, r_k, lnx_g, lnx_b):
    B, T, _ = z.shape
    f32 = jnp.float32
    z = z.astype(f32)
    hd = lambda t_: t_.reshape(t_.shape[:-1] + (N_HEADS, HEAD_DIM))
    r = hd(z[..., 0:D_RWKV])
    k = hd(z[..., D_RWKV:2 * D_RWKV])
    v = hd(z[..., 2 * D_RWKV:3 * D_RWKV])
    o = 3 * D_RWKV
    zw = (z[..., o:o + R_DECAY], z[..., o + R_DECAY:o + 2 * R_DECAY])
    o += 2 * R_DECAY
    za = (z[..., o:o + R_AAA], z[..., o + R_AAA:o + 2 * R_AAA])
    o += 2 * R_AAA
    zg = z[..., o:o + R_GATE]
    kkh = hd(k_k.astype(f32))
    kah = hd(k_a.astype(f32))
    rkh = hd(r_k.astype(f32))
    kk = k * kkh
    kk = kk / jnp.maximum(jnp.sqrt(jnp.sum(kk * kk, axis=-1, keepdims=True)), L2_EPS)
    ys = []
    bonus = []
    for d in range(N_DIR):
        w_log = -jax.nn.softplus(-(w0[d].astype(f32) + jnp.tanh(zw[d]) @ w2[d].astype(f32))) - 0.5
        decay = hd(jnp.exp(-jnp.exp(w_log)))
        a = hd(jax.nn.sigmoid(a0[d].astype(f32) + za[d] @ a2[d].astype(f32)))
        k_d = k * (1 + (a - 1) * kah)
        ys.append(wkv_scan(r, decay, k_d, v, kk, a, reverse=(d == 1)))
        bonus.append(jnp.sum(r * k_d * rkh, axis=-1, keepdims=True) * v)
    y = ys[0] + ys[1]
    mean = jnp.mean(y, axis=-1, keepdims=True)
    var = jnp.mean(jnp.square(y - mean), axis=-1, keepdims=True)
    y = (y - mean) * lax.rsqrt(var + GN_EPS) * hd(lnx_g.astype(f32)) + hd(lnx_b.astype(f32))
    g = jax.nn.sigmoid(zg) @ g2.astype(f32)
    return (y + bonus[0] + bonus[1]).reshape(B, T, D_RWKV) * g


def encoder_layer(x, c, ada_w, ada_b, n1_pre, n1_post, f1_w1, f1_w3, f1_w2,
                  nm_pre, nm_post, w_in, shift_mu, pool_w, pool_scale,
                  w0, w2, a0, a2, g2, k_k, k_a, r_k, lnx_g, lnx_b, w_out,
                  n2_pre, n2_post, f2_w1, f2_w3, f2_w2):
    B = x.shape[0]
    mod = (jax.nn.silu(c) @ ada_w + ada_b).reshape(B, N_MOD, D_MODEL)[:, :, None, :]
    h = modulate(rmsnorm(x, n1_pre), mod[:, 0], mod[:, 1])
    x = x + 0.5 * mod[:, 2] * rmsnorm(swiglu(h, f1_w1, f1_w3, f1_w2), n1_post)
    h = modulate(rmsnorm(x, nm_pre), mod[:, 3], mod[:, 4])
    z = h @ w_in
    zp = z[..., :D_POOL]
    zr = token_shift(z[..., D_POOL:], shift_mu)
    m = jnp.concatenate([pool_mixer(zp, pool_w, pool_scale),
                         rwkv7_mixer(zr, w0, w2, a0, a2, g2, k_k, k_a, r_k, lnx_g, lnx_b)], axis=-1)
    m = m.astype(x.dtype) @ w_out
    x = x + mod[:, 5] * rmsnorm(m, nm_post)
    h = modulate(rmsnorm(x, n2_pre), mod[:, 6], mod[:, 7])
    x = x + 0.5 * mod[:, 8] * rmsnorm(swiglu(h, f2_w1, f2_w3, f2_w2), n2_post)
    return x


def encoder_trunk(x, c, params):
    for i in range(DEPTH):
        x = encoder_layer(x, c, *[p[i] for p in params])
    return x


def setup_inputs(seed: int = 0) -> dict:
    key = jax.random.key(seed)
    ks = iter(jax.random.split(key, 48))
    f32 = jnp.float32
    nrm = lambda shape, scale: scale * jax.random.normal(next(ks), shape, f32)
    L, D, F = DEPTH, D_MODEL, D_FF
    return {
        "x_prompt": nrm((BATCH, SEQ, D), 1.0),
        "x_sample": nrm((DEC_BATCH, DEC_SEQ, D), 1.0),
        "c_prompt": nrm((BATCH, D), 1.0),
        "c_sample": nrm((DEC_BATCH, D), 1.0),
        "ada_w": nrm((L, D, N_MOD * D), 0.5 * D ** -0.5),
        "ada_b": nrm((L, N_MOD * D), 0.01),
        "n1_pre": 1.0 + nrm((L, D), 0.05),
        "n1_post": 1.0 + nrm((L, D), 0.05),
        "f1_w1": nrm((L, D, F), D ** -0.5),
        "f1_w3": nrm((L, D, F), D ** -0.5),
        "f1_w2": nrm((L, F, D), F ** -0.5),
        "nm_pre": 1.0 + nrm((L, D), 0.05),
        "nm_post": 1.0 + nrm((L, D), 0.05),
        "w_in": nrm((L, D, P_IN), D ** -0.5),
        "shift_mu": jax.random.uniform(next(ks), (L, P_RWKV), f32),
        "pool_w": nrm((L, N_POOL_GROUPS, POOL_GROUP, POOL_GROUP), POOL_GROUP ** -0.5),
        "pool_scale": 1.0 + nrm((L, D_POOL), 0.1),
        "w0": -1.0 + nrm((L, N_DIR, D_RWKV), 0.5),
        "w2": nrm((L, N_DIR, R_DECAY, D_RWKV), R_DECAY ** -0.5),
        "a0": nrm((L, N_DIR, D_RWKV), 0.5),
        "a2": nrm((L, N_DIR, R_AAA, D_RWKV), 0.5 * R_AAA ** -0.5),
        "g2": nrm((L, R_GATE, D_RWKV), R_GATE ** -0.5),
        "k_k": 0.85 + nrm((L, D_RWKV), 0.05),
        "k_a": 1.0 + nrm((L, D_RWKV), 0.05),
        "r_k": nrm((L, D_RWKV), 0.5),
        "lnx_g": 1.0 + nrm((L, D_RWKV), 0.05),
        "lnx_b": nrm((L, D_RWKV), 0.01),
        "w_out": nrm((L, D, D), D ** -0.5),
        "n2_pre": 1.0 + nrm((L, D), 0.05),
        "n2_post": 1.0 + nrm((L, D), 0.05),
        "f2_w1": nrm((L, D, F), D ** -0.5),
        "f2_w3": nrm((L, D, F), D ** -0.5),
        "f2_w2": nrm((L, F, D), F ** -0.5),
    }


def reference(x_prompt, x_sample, c_prompt, c_sample, ada_w, ada_b, n1_pre, n1_post, f1_w1, f1_w3, f1_w2,
              nm_pre, nm_post, w_in, shift_mu, pool_w, pool_scale,
              w0, w2, a0, a2, g2, k_k, k_a, r_k, lnx_g, lnx_b, w_out,
              n2_pre, n2_post, f2_w1, f2_w3, f2_w2):
    params = (ada_w, ada_b, n1_pre, n1_post, f1_w1, f1_w3, f1_w2,
              nm_pre, nm_post, w_in, shift_mu, pool_w, pool_scale,
              w0, w2, a0, a2, g2, k_k, k_a, r_k, lnx_g, lnx_b, w_out,
              n2_pre, n2_post, f2_w1, f2_w3, f2_w2)
    y_prompt = encoder_trunk(x_prompt, c_prompt, params)
    y_sample = encoder_trunk(x_sample, c_sample, params)
    return (y_prompt, y_sample)
```

```python
import functools

import numpy as np
import jax
import jax.numpy as jnp
from jax import lax
from jax.experimental import pallas as pl
from jax.experimental.pallas import tpu as pltpu

F32 = jnp.float32
BF16 = jnp.bfloat16

D_MODEL = 1024
D_POOL = 512
D_RWKV = 512
HEAD_DIM = 64
N_PAIRS = D_RWKV // (2 * HEAD_DIM)
POOL_WINDOWS = (2, 4, 8, 16)
POOL_GROUP = 128
R_GATE = 160
N_MOD = 9
RMS_EPS = 1e-6
GN_EPS = 64e-5
L2_EPS = 1e-12

Z_RWKV = 2048
Z_WIDTH = Z_RWKV + D_POOL
LANE = 128
SUBLANE = 8
CHUNK = 64
HALO = SUBLANE

ROW_TILE = 256
WKV_TILE = 256
F_CHUNKS = ((0, 768), (768, 1536), (1536, 2304), (2304, 2816))
VMEM_LIMIT = 56 * 1024 * 1024


def _rms(x, g):
    ms = jnp.mean(x * x, axis=-1, keepdims=True)
    return x * lax.rsqrt(ms + RMS_EPS) * g


def _dot(a, b):
    return jnp.dot(a, b, preferred_element_type=F32)


def _dot_nt(a, b):
    return lax.dot_general(a, b, (((1,), (1,)), ((), ())), preferred_element_type=F32)


def _split2(x):
    hi = x.astype(BF16)
    lo = (x - hi.astype(F32)).astype(BF16)
    return hi, lo


def _split3(x):
    hi = x.astype(BF16)
    r1 = x - hi.astype(F32)
    mid = r1.astype(BF16)
    lo = (r1 - mid.astype(F32)).astype(BF16)
    return hi, mid, lo


def _headsum(x, ones_ref):
    hi, lo = _split2(x)
    ones = ones_ref[...]
    return _dot(hi, ones) + _dot(lo, ones)


def _softplus(x):
    return jnp.maximum(x, 0.0) + jnp.log(1.0 + jnp.exp(-jnp.abs(x)))


def _token_shift(z, prev_row, next_row, mu):
    n = z.shape[0]
    row = lax.broadcasted_iota(jnp.int32, z.shape, 0)
    prev = jnp.where(row == 0, prev_row, pltpu.roll(z, 1, 0))
    nxt = jnp.where(row == n - 1, next_row, pltpu.roll(z, n - 1, 0))
    return z + (0.5 * (prev + nxt) - z) * mu


def _ffn(x, mod_ref, j, pre_ref, post_ref, w1_ref, w3_ref, w2_ref):
    shift = mod_ref[0, j:j + 1, :]
    scale = mod_ref[0, j + 1:j + 2, :]
    gate = mod_ref[0, j + 2:j + 3, :]
    h = (_rms(x, pre_ref[...]) * (1.0 + scale) + shift).astype(BF16)
    acc = None
    for lo, hi in F_CHUNKS:
        a = _dot(h, w1_ref[:, lo:hi])
        b = _dot(h, w3_ref[:, lo:hi])
        g = (a * jax.nn.sigmoid(a) * b).astype(BF16)
        p = _dot(g, w2_ref[lo:hi, :])
        acc = p if acc is None else acc + p
    return x + 0.5 * gate * _rms(acc, post_ref[...])


def _mod_kernel(c_ref, w_ref, b_ref, o_ref):
    c = c_ref[...]
    s = (c * jax.nn.sigmoid(c)).astype(BF16)
    o_ref[...] = _dot(s, w_ref[...].astype(BF16)) + b_ref[...]


def _mod_call(c, ada_w, ada_b):
    nb = c.shape[0]
    n = ada_w.shape[1]
    bn = D_MODEL
    return pl.pallas_call(
        _mod_kernel,
        grid=(n // bn,),
        in_specs=[
            pl.BlockSpec((nb, D_MODEL), lambda j: (0, 0)),
            pl.BlockSpec((D_MODEL, bn), lambda j: (0, j)),
            pl.BlockSpec((1, bn), lambda j: (0, j)),
        ],
        out_specs=pl.BlockSpec((nb, bn), lambda j: (0, j)),
        out_shape=jax.ShapeDtypeStruct((nb, n), F32),
        compiler_params=pltpu.CompilerParams(
            dimension_semantics=("arbitrary",), vmem_limit_bytes=VMEM_LIMIT),
        name="mod",
    )(c, ada_w, ada_b.reshape(1, n))


def _ffn_in_kernel(x_ref, mod_ref, pre_ref, post_ref, w1_ref, w3_ref, w2_ref,
                   nmpre_ref, win_ref, x1_ref, z_ref):
    x1 = _ffn(x_ref[0], mod_ref, 0, pre_ref, post_ref, w1_ref, w3_ref, w2_ref)
    x1_ref[0] = x1
    shift = mod_ref[0, 3:4, :]
    scale = mod_ref[0, 4:5, :]
    h = (_rms(x1, nmpre_ref[...]) * (1.0 + scale) + shift).astype(BF16)
    z_ref[0] = _dot(h, win_ref[...])


def _const_spec(shape):
    nd = len(shape)
    return pl.BlockSpec(shape, lambda *_: (0,) * nd, pipeline_mode=pl.Buffered(1))


def _ffn_in_call(x, mod, n_pre, n_post, w1, w3, w2, nm_pre, w_in):
    nb, t, d = x.shape
    f = w1.shape[1]
    tm = ROW_TILE
    row = lambda b, i: (b, i, 0)
    return pl.pallas_call(
        _ffn_in_kernel,
        grid=(nb, t // tm),
        in_specs=[
            pl.BlockSpec((1, tm, d), row),
            pl.BlockSpec((1, N_MOD, d), lambda b, i: (b, 0, 0)),
            _const_spec((1, d)), _const_spec((1, d)),
            _const_spec((d, f)), _const_spec((d, f)), _const_spec((f, d)),
            _const_spec((1, d)), _const_spec((d, Z_WIDTH)),
        ],
        out_specs=[pl.BlockSpec((1, tm, d), row), pl.BlockSpec((1, tm, Z_WIDTH), row)],
        out_shape=[jax.ShapeDtypeStruct((nb, t, d), F32),
                   jax.ShapeDtypeStruct((nb, t, Z_WIDTH), F32)],
        compiler_params=pltpu.CompilerParams(
            dimension_semantics=("parallel", "parallel"), vmem_limit_bytes=VMEM_LIMIT),
        name="ffn_in",
    )(x, mod, n_pre, n_post, w1, w3, w2, nm_pre, w_in)


def _bd(x, bdmask):
    return jnp.where(bdmask, jnp.concatenate([x, x], axis=0), 0.0)


def _wkv_kernel(reverse, with_bonus,
                z_ref, zp_ref, zn_ref, mu_ref, tri_ref, ones_ref,
                w0_ref, w2_ref, a0_ref, a2_ref, kk_ref, ka_ref, rk_ref, *rest):
    if with_bonus:
        a0b_ref, a2b_ref, y_ref, bonus_ref = rest[:4]
        scratch = rest[4:]
    else:
        y_ref = rest[0]
        scratch = rest[1:]
    at_ref, rt_ref, kt_ref, bt_ref, v_ref, kbt_ref, gl_ref, h_ref = scratch

    ti = pl.program_id(1)
    nt = pl.num_programs(1)
    tb = z_ref.shape[1]
    nc = tb // CHUNK

    @pl.when(ti == 0)
    def _():
        h_ref[...] = jnp.zeros_like(h_ref)

    tt = (nt - 1 - ti) if reverse else ti
    has_prev = (tt > 0).astype(F32)
    has_next = (tt < nt - 1).astype(F32)

    z = z_ref[0]
    mu = mu_ref[...]
    zs = _token_shift(z, zp_ref[0, HALO - 1:HALO, :] * has_prev, zn_ref[0, 0:1, :] * has_next, mu)
    r = zs[:, 0:512]
    k = zs[:, 512:1024]
    v = zs[:, 1024:1536]
    zw = zs[:, 1536:1664]
    za = zs[:, 1664:1792]

    wl = w0_ref[...] + _dot(jnp.tanh(zw).astype(BF16), w2_ref[...])
    w_log = -_softplus(-wl) - 0.5
    lw = -jnp.exp(w_log)
    za16 = za.astype(BF16)
    a = jax.nn.sigmoid(a0_ref[...] + _dot(za16, a2_ref[...]))

    kk = k * kk_ref[...]
    kkn = kk / jnp.maximum(jnp.sqrt(_headsum(kk * kk, ones_ref)), L2_EPS)
    ka = ka_ref[...]
    k_d = k * (1.0 + (a - 1.0) * ka)
    b = kkn * a

    if with_bonus:
        a_o = jax.nn.sigmoid(a0b_ref[...] + _dot(za16, a2b_ref[...]))
        k_sum = k_d + k * (1.0 + (a_o - 1.0) * ka)
        bonus_ref[0] = _headsum(r * k_sum * rk_ref[...], ones_ref) * v

    tri = tri_ref[...]
    hi, mid, lo = _split3(lw)
    cc = _dot(tri, hi) + _dot(tri, mid) + _dot(tri, lo)
    c = cc[0:tb]
    ctot = cc[tb:2 * tb]
    e = c - lw
    at_ref[...] = (-kkn * jnp.exp(e)).astype(BF16)
    rt_ref[...] = (r * jnp.exp(c)).astype(BF16)
    einv = jnp.exp(-c)
    kt_ref[...] = (k_d * einv).astype(BF16)
    bt_ref[...] = (b * einv).astype(BF16)
    v_ref[...] = v.astype(BF16)
    erem = jnp.exp(ctot - c)
    kh = k_d * erem
    bh = b * erem
    for ci in range(nc):
        rows = slice(ci * CHUNK, (ci + 1) * CHUNK)
        kb = jnp.concatenate([kh[rows], bh[rows]], axis=0)
        kbt_ref[ci] = kb.T.astype(BF16)
    gl_ref[...] = jnp.exp(ctot)

    lane128 = lax.broadcasted_iota(jnp.int32, (CHUNK, LANE), 1)
    row64 = lax.broadcasted_iota(jnp.int32, (CHUNK, LANE), 0)
    even_m = lane128 < HEAD_DIM
    ipair = jnp.where((lane128 % HEAD_DIM) == row64, 1.0, 0.0).astype(F32)
    r128 = lax.broadcasted_iota(jnp.int32, (LANE, LANE), 0)
    c128 = lax.broadcasted_iota(jnp.int32, (LANE, LANE), 1)
    bdmask = (r128 < HEAD_DIM) == (c128 < HEAD_DIM)
    eye128 = r128 == c128
    sr = lax.broadcasted_iota(jnp.int32, (2 * CHUNK, 2 * LANE), 0)
    sc = lax.broadcasted_iota(jnp.int32, (2 * CHUNK, 2 * LANE), 1)
    si = sr % CHUNK
    ss = sc % CHUNK
    incl = jnp.where(sr < CHUNK, 0, 1)
    trimask = (ss > si - incl) if reverse else (ss < si + incl)
    bdmask2 = jnp.concatenate([bdmask, bdmask], axis=1)
    zeros_pair = jnp.zeros((CHUNK, LANE), F32)
    zeros_bd = jnp.zeros((LANE, LANE), F32)

    def chunk_body(j, carry):
        ci = (nc - 1 - j) if reverse else j
        rows = pl.ds(pl.multiple_of(ci * CHUNK, CHUNK), CHUNK)
        for p in range(N_PAIRS):
            lanes = slice(p * LANE, (p + 1) * LANE)
            at = at_ref[rows, lanes]
            rt = rt_ref[rows, lanes]
            kt = kt_ref[rows, lanes]
            bt = bt_ref[rows, lanes]
            vv = v_ref[rows, lanes].astype(F32)
            ktf = kt.astype(F32)
            btf = bt.astype(F32)
            lhs1 = jnp.concatenate([at, rt], axis=0)
            rhsm = jnp.concatenate([
                jnp.where(even_m, ktf, 0.0), jnp.where(even_m, 0.0, ktf),
                jnp.where(even_m, btf, 0.0), jnp.where(even_m, 0.0, btf)], axis=0).astype(BF16)
            s = jnp.where(trimask, _dot_nt(lhs1, rhsm), 0.0)
            ak = s[0:CHUNK, 0:LANE]
            nk = s[0:CHUNK, LANE:2 * LANE]
            rkrb = s[CHUNK:2 * CHUNK, :]
            x = ipair + nk
            for _ in range(5):
                nk = _dot(nk.astype(BF16), _bd(nk, bdmask).astype(BF16))
                x = x + _dot(x.astype(BF16), _bd(nk, bdmask).astype(BF16))
            tm = x.astype(BF16)
            bdv = _bd(vv, bdmask)
            g = _dot(ak.astype(BF16), bdv.astype(BF16))
            au = _dot(tm, jnp.concatenate(
                [_bd(at.astype(F32), bdmask), _bd(g, bdmask)], axis=1).astype(BF16))
            a1 = au[:, 0:LANE]
            u0 = au[:, LANE:2 * LANE]
            rhs2 = jnp.concatenate([
                jnp.concatenate([zeros_bd, bdv], axis=1),
                jnp.concatenate([_bd(a1, bdmask), _bd(u0, bdmask)], axis=1)], axis=0)
            ry = _dot(rkrb.astype(BF16), rhs2.astype(BF16))
            r1 = rt.astype(F32) + ry[:, 0:LANE]
            y0 = ry[:, LANE:2 * LANE]
            rhs3 = jnp.concatenate([
                jnp.concatenate([zeros_pair, vv], axis=1),
                jnp.concatenate([a1, u0], axis=1)], axis=0)
            kbt = kbt_ref[ci, p * LANE:(p + 1) * LANE, :]
            pq = jnp.where(bdmask2, _dot(kbt, rhs3.astype(BF16)), 0.0)
            gl = gl_ref[pl.ds(ci * CHUNK, 1), lanes]
            pbd = pq[:, 0:LANE] + jnp.where(eye128, gl, 0.0)
            qbd = pq[:, LANE:2 * LANE]
            h = h_ref[p]
            zz = _dot(jnp.concatenate([r1, pbd], axis=0).astype(BF16), h.astype(BF16))
            y_ref[0, rows, lanes] = zz[0:CHUNK] + y0
            h_ref[p] = zz[CHUNK:] + qbd
        return carry

    lax.fori_loop(0, nc, chunk_body, 0)


def _wkv_call(reverse, z, mu_r, tri, ones, w0, w2p, a0, a2p, k_k, k_a, r_k, bonus_args=None):
    nb, t, _ = z.shape
    tb = WKV_TILE
    nt = t // tb
    nhalo = tb // HALO
    last_halo = t // HALO - 1
    with_bonus = bonus_args is not None

    def tsel(i):
        return (nt - 1 - i) if reverse else i

    main = lambda b, i: (b, tsel(i), 0)
    prev = lambda b, i: (b, jnp.maximum(tsel(i) * nhalo - 1, 0), 0)
    nxt = lambda b, i: (b, jnp.minimum((tsel(i) + 1) * nhalo, last_halo), 0)

    in_specs = [
        pl.BlockSpec((1, tb, Z_RWKV), main),
        pl.BlockSpec((1, HALO, Z_RWKV), prev),
        pl.BlockSpec((1, HALO, Z_RWKV), nxt),
        _const_spec((1, Z_RWKV)),
        _const_spec((2 * tb, tb)),
        _const_spec((D_RWKV, D_RWKV)),
        _const_spec((1, D_RWKV)), _const_spec((LANE, D_RWKV)),
        _const_spec((1, D_RWKV)), _const_spec((LANE, D_RWKV)),
        _const_spec((1, D_RWKV)), _const_spec((1, D_RWKV)), _const_spec((1, D_RWKV)),
    ]
    args = [z, z, z, mu_r, tri, ones, w0, w2p, a0, a2p, k_k, k_a, r_k]
    out_specs = [pl.BlockSpec((1, tb, D_RWKV), main)]
    out_shape = [jax.ShapeDtypeStruct((nb, t, D_RWKV), F32)]
    if with_bonus:
        in_specs += [_const_spec((1, D_RWKV)), _const_spec((LANE, D_RWKV))]
        args += list(bonus_args)
        out_specs.append(pl.BlockSpec((1, tb, D_RWKV), main))
        out_shape.append(jax.ShapeDtypeStruct((nb, t, D_RWKV), F32))

    nc = tb // CHUNK
    scratch = [pltpu.VMEM((tb, D_RWKV), BF16) for _ in range(5)]
    scratch += [pltpu.VMEM((nc, D_RWKV, LANE), BF16),
                pltpu.VMEM((tb, D_RWKV), F32),
                pltpu.VMEM((N_PAIRS, LANE, LANE), F32)]
    return pl.pallas_call(
        functools.partial(_wkv_kernel, reverse, with_bonus),
        grid=(nb, nt),
        in_specs=in_specs,
        out_specs=out_specs,
        out_shape=out_shape,
        scratch_shapes=scratch,
        compiler_params=pltpu.CompilerParams(
            dimension_semantics=("parallel", "arbitrary"), vmem_limit_bytes=VMEM_LIMIT),
        name="wkv_bwd" if reverse else "wkv_fwd",
    )(*args)


def _mix_out_kernel(x1_ref, mod_ref, y0_ref, y1_ref, bonus_ref,
                    zg_ref, zgp_ref, zgn_ref, zq_ref, zqp_ref, zqn_ref,
                    mug_ref, ones_ref, g2_ref, lng_ref, lnb_ref, pw_ref, ps_ref,
                    wout_ref, nmpost_ref, pre_ref, post_ref, w1_ref, w3_ref, w2_ref, o_ref):
    ti = pl.program_id(1)
    nt = pl.num_programs(1)
    tm = x1_ref.shape[1]
    has_prev = (ti > 0).astype(F32)
    has_next = (ti < nt - 1).astype(F32)

    y = y0_ref[0] + y1_ref[0]
    inv_n = 1.0 / HEAD_DIM
    mean = _headsum(y, ones_ref) * inv_n
    dlt = y - mean
    var = _headsum(dlt * dlt, ones_ref) * inv_n
    yn = dlt * lax.rsqrt(var + GN_EPS) * lng_ref[...] + lnb_ref[...]
    zg = _token_shift(zg_ref[0], zgp_ref[0, HALO - 1:HALO, :] * has_prev,
                      zgn_ref[0, 0:1, :] * has_next, mug_ref[...])
    gate = _dot(jax.nn.sigmoid(zg).astype(BF16), g2_ref[...])
    m_r = ((yn + bonus_ref[0]) * gate).astype(BF16)

    ext = jnp.concatenate([zqp_ref[0] * has_prev, zq_ref[0], zqn_ref[0] * has_next], axis=0)
    n = ext.shape[0]
    tglob = ti * tm + lax.broadcasted_iota(jnp.int32, (tm, POOL_GROUP), 0)
    t_total = nt * tm
    cur = ext
    m_out = _dot(m_r, wout_ref[D_POOL:D_MODEL, :])
    for gi, win in enumerate(POOL_WINDOWS):
        if gi == 0:
            cur = cur + pltpu.roll(cur, 1, 0)
        else:
            sh = win // 4
            cur = cur[:, POOL_GROUP:]
            cur = pltpu.roll(cur, n - sh, 0) + pltpu.roll(cur, sh, 0)
        wsum = cur[HALO:HALO + tm, 0:POOL_GROUP]
        half = win // 2
        cnt = (jnp.minimum(tglob + half, t_total) - jnp.maximum(tglob - half, 0)).astype(F32)
        zc = ext[HALO:HALO + tm, gi * POOL_GROUP:(gi + 1) * POOL_GROUP]
        pg = (wsum / cnt - zc).astype(BF16)
        po = _dot(pg, pw_ref[gi]) * ps_ref[:, gi * POOL_GROUP:(gi + 1) * POOL_GROUP]
        m_out = m_out + _dot(po.astype(BF16), wout_ref[gi * POOL_GROUP:(gi + 1) * POOL_GROUP, :])

    x2 = x1_ref[0] + mod_ref[0, 5:6, :] * _rms(m_out, nmpost_ref[...])
    o_ref[0] = _ffn(x2, mod_ref, 6, pre_ref, post_ref, w1_ref, w3_ref, w2_ref)


def _mix_out_call(x1, mod, y0, y1, bonus, z, mu_g, ones, g2p, lnx_g, lnx_b, pool_w, pool_scale,
                  w_out, nm_post, n_pre, n_post, w1, w3, w2):
    nb, t, d = x1.shape
    f = w1.shape[1]
    tm = ROW_TILE
    nhalo = tm // HALO
    last_halo = t // HALO - 1
    gw = 2 * LANE
    gblk = Z_RWKV // gw - 1
    qblk = Z_RWKV // D_POOL
    row = lambda b, i: (b, i, 0)

    def col(blk):
        return (lambda b, i: (b, i, blk),
                lambda b, i: (b, jnp.maximum(i * nhalo - 1, 0), blk),
                lambda b, i: (b, jnp.minimum((i + 1) * nhalo, last_halo), blk))

    g_main, g_prev, g_next = col(gblk)
    q_main, q_prev, q_next = col(qblk)
    return pl.pallas_call(
        _mix_out_kernel,
        grid=(nb, t // tm),
        in_specs=[
            pl.BlockSpec((1, tm, d), row),
            pl.BlockSpec((1, N_MOD, d), lambda b, i: (b, 0, 0)),
            pl.BlockSpec((1, tm, D_RWKV), row),
            pl.BlockSpec((1, tm, D_RWKV), row),
            pl.BlockSpec((1, tm, D_RWKV), row),
            pl.BlockSpec((1, tm, gw), g_main),
            pl.BlockSpec((1, HALO, gw), g_prev),
            pl.BlockSpec((1, HALO, gw), g_next),
            pl.BlockSpec((1, tm, D_POOL), q_main),
            pl.BlockSpec((1, HALO, D_POOL), q_prev),
            pl.BlockSpec((1, HALO, D_POOL), q_next),
            _const_spec((1, gw)), _const_spec((D_RWKV, D_RWKV)), _const_spec((gw, D_RWKV)),
            _const_spec((1, D_RWKV)), _const_spec((1, D_RWKV)),
            _const_spec((len(POOL_WINDOWS), POOL_GROUP, POOL_GROUP)), _const_spec((1, D_POOL)),
            _const_spec((d, d)), _const_spec((1, d)),
            _const_spec((1, d)), _const_spec((1, d)),
            _const_spec((d, f)), _const_spec((d, f)), _const_spec((f, d)),
        ],
        out_specs=pl.BlockSpec((1, tm, d), row),
        out_shape=jax.ShapeDtypeStruct((nb, t, d), F32),
        compiler_params=pltpu.CompilerParams(
            dimension_semantics=("parallel", "parallel"), vmem_limit_bytes=VMEM_LIMIT),
        name="mix_out",
    )(x1, mod, y0, y1, bonus, z, z, z, z, z, z, mu_g, ones, g2p, lnx_g, lnx_b, pool_w, pool_scale,
      w_out, nm_post, n_pre, n_post, w1, w3, w2)


def _tri_consts(tb):
    i = np.arange(tb)[:, None]
    j = np.arange(tb)[None, :]
    same = (i // CHUNK) == (j // CHUNK)
    fwd = np.concatenate([same & (j <= i), same], axis=0)
    bwd = np.concatenate([same & (j >= i), same], axis=0)
    return jnp.asarray(fwd, BF16), jnp.asarray(bwd, BF16)


def _head_ones():
    i = np.arange(D_RWKV)
    return jnp.asarray((i[:, None] // HEAD_DIM) == (i[None, :] // HEAD_DIM), BF16)


def _layer(x, c_mod, p):
    x1, z = _ffn_in_call(x, c_mod, p["n1_pre"], p["n1_post"], p["f1_w1"], p["f1_w3"], p["f1_w2"],
                         p["nm_pre"], p["w_in"])
    y0, bonus = _wkv_call(False, z, p["mu_r"], p["tri_f"], p["ones"], p["w0"][0], p["w2p"][0],
                          p["a0"][0], p["a2p"][0], p["k_k"], p["k_a"], p["r_k"],
                          bonus_args=(p["a0"][1], p["a2p"][1]))
    (y1,) = _wkv_call(True, z, p["mu_r"], p["tri_b"], p["ones"], p["w0"][1], p["w2p"][1],
                      p["a0"][1], p["a2p"][1], p["k_k"], p["k_a"], p["r_k"])
    return _mix_out_call(x1, c_mod, y0, y1, bonus, z, p["mu_g"], p["ones"], p["g2p"],
                         p["lnx_g"], p["lnx_b"], p["pool_w"], p["pool_scale"], p["w_out"],
                         p["nm_post"], p["n2_pre"], p["n2_post"], p["f2_w1"], p["f2_w3"], p["f2_w2"])


def _prepare(i, n1_pre, n1_post, f1_w1, f1_w3, f1_w2, nm_pre, nm_post, w_in, shift_mu, pool_w,
             pool_scale, w0, w2, a0, a2, g2, k_k, k_a, r_k, lnx_g, lnx_b, w_out,
             n2_pre, n2_post, f2_w1, f2_w3, f2_w2):
    row = lambda a: a[i].reshape(1, -1)
    n_real = w_in.shape[2] - D_POOL
    pad = Z_RWKV - n_real
    w_in_i = w_in[i]
    win = jnp.concatenate(
        [w_in_i[:, D_POOL:], jnp.zeros((D_MODEL, pad), F32), w_in_i[:, :D_POOL]], axis=1)
    mu_r = jnp.concatenate([shift_mu[i], jnp.zeros((pad,), F32)]).reshape(1, Z_RWKV)
    half = LANE // 2
    zpad = jnp.zeros((half, D_RWKV), F32)
    w2p = [jnp.concatenate([w2[i, 0], zpad], 0), jnp.concatenate([zpad, w2[i, 1]], 0)]
    a2p = [jnp.concatenate([a2[i, 0], zpad], 0), jnp.concatenate([zpad, a2[i, 1]], 0)]
    g2p = jnp.concatenate([g2[i], jnp.zeros((2 * LANE - R_GATE, D_RWKV), F32)], 0)
    tri_f, tri_b = _tri_consts(WKV_TILE)
    return dict(
        n1_pre=row(n1_pre), n1_post=row(n1_post),
        f1_w1=f1_w1[i].astype(BF16), f1_w3=f1_w3[i].astype(BF16), f1_w2=f1_w2[i].astype(BF16),
        nm_pre=row(nm_pre), nm_post=row(nm_post), w_in=win.astype(BF16),
        mu_r=mu_r, mu_g=mu_r[:, Z_RWKV - 2 * LANE:],
        pool_w=pool_w[i].astype(BF16), pool_scale=row(pool_scale),
        w0=[w0[i, d].reshape(1, -1) for d in range(2)], w2p=[m.astype(BF16) for m in w2p],
        a0=[a0[i, d].reshape(1, -1) for d in range(2)], a2p=[m.astype(BF16) for m in a2p],
        g2p=g2p.astype(BF16), k_k=row(k_k), k_a=row(k_a), r_k=row(r_k),
        lnx_g=row(lnx_g), lnx_b=row(lnx_b), w_out=w_out[i].astype(BF16),
        n2_pre=row(n2_pre), n2_post=row(n2_post),
        f2_w1=f2_w1[i].astype(BF16), f2_w3=f2_w3[i].astype(BF16), f2_w2=f2_w2[i].astype(BF16),
        tri_f=tri_f, tri_b=tri_b, ones=_head_ones(),
    )


def kernel(x_prompt, x_sample, c_prompt, c_sample, ada_w, ada_b, n1_pre, n1_post, f1_w1, f1_w3, f1_w2, nm_pre, nm_post, w_in, shift_mu, pool_w, pool_scale, w0, w2, a0, a2, g2, k_k, k_a, r_k, lnx_g, lnx_b, w_out, n2_pre, n2_post, f2_w1, f2_w3, f2_w2):
    depth = ada_w.shape[0]
    nbp = x_prompt.shape[0]
    xs = [x_prompt, x_sample]
    c_all = jnp.concatenate([c_prompt, c_sample], axis=0)
    for i in range(depth):
        p = _prepare(i, n1_pre, n1_post, f1_w1, f1_w3, f1_w2, nm_pre, nm_post, w_in, shift_mu,
                     pool_w, pool_scale, w0, w2, a0, a2, g2, k_k, k_a, r_k, lnx_g, lnx_b, w_out,
                     n2_pre, n2_post, f2_w1, f2_w3, f2_w2)
        mod = _mod_call(c_all, ada_w[i], ada_b[i]).reshape(c_all.shape[0], N_MOD, D_MODEL)
        mods = [mod[:nbp], mod[nbp:]]
        xs = [_layer(x, m, p) for x, m in zip(xs, mods)]
    return (xs[0], xs[1])
```

```python
import functools

import numpy as np
import jax
import jax.numpy as jnp
from jax import lax
from jax.experimental import pallas as pl
from jax.experimental.pallas import tpu as pltpu

F32 = jnp.float32
BF16 = jnp.bfloat16

D_MODEL = 1024
D_POOL = 512
D_RWKV = 512
HEAD_DIM = 64
N_PAIRS = D_RWKV // (2 * HEAD_DIM)
POOL_WINDOWS = (2, 4, 8, 16)
POOL_GROUP = 128
R_GATE = 160
N_MOD = 9
RMS_EPS = 1e-6
GN_EPS = 64e-5
L2_EPS = 1e-12

Z_RWKV = 2048
Z_WIDTH = Z_RWKV + D_POOL
LANE = 128
SUBLANE = 8
CHUNK = 64
HALO = SUBLANE

ROW_TILE = 256
WKV_BATCH = 4
F_CHUNKS = ((0, 768), (768, 1536), (1536, 2304), (2304, 2816))
VMEM_LIMIT = 56 * 1024 * 1024


def _rms(x, g):
    ms = jnp.mean(x * x, axis=-1, keepdims=True)
    return x * lax.rsqrt(ms + RMS_EPS) * g


def _dot(a, b):
    return jnp.dot(a, b, preferred_element_type=F32)


def _dot_nt(a, b):
    return lax.dot_general(a, b, (((1,), (1,)), ((), ())), preferred_element_type=F32)


def _split2(x):
    hi = x.astype(BF16)
    lo = (x - hi.astype(F32)).astype(BF16)
    return hi, lo


def _split3(x):
    hi = x.astype(BF16)
    r1 = x - hi.astype(F32)
    mid = r1.astype(BF16)
    lo = (r1 - mid.astype(F32)).astype(BF16)
    return hi, mid, lo


def _headsum(x, ones_ref):
    hi, lo = _split2(x)
    ones = ones_ref[...]
    return _dot(hi, ones) + _dot(lo, ones)


def _softplus(x):
    return jnp.maximum(x, 0.0) + jnp.log(1.0 + jnp.exp(-jnp.abs(x)))


def _token_shift(z, prev_row, next_row, mu):
    n = z.shape[0]
    row = lax.broadcasted_iota(jnp.int32, z.shape, 0)
    prev = jnp.where(row == 0, prev_row, pltpu.roll(z, 1, 0))
    nxt = jnp.where(row == n - 1, next_row, pltpu.roll(z, n - 1, 0))
    return z + (0.5 * (prev + nxt) - z) * mu


def _ffn(x, mod_ref, j, pre_ref, post_ref, w1_ref, w3_ref, w2_ref):
    shift = mod_ref[0, j:j + 1, :]
    scale = mod_ref[0, j + 1:j + 2, :]
    gate = mod_ref[0, j + 2:j + 3, :]
    h = (_rms(x, pre_ref[...]) * (1.0 + scale) + shift).astype(BF16)
    acc = None
    for lo, hi in F_CHUNKS:
        a = _dot(h, w1_ref[:, lo:hi])
        b = _dot(h, w3_ref[:, lo:hi])
        g = (a * jax.nn.sigmoid(a) * b).astype(BF16)
        p = _dot(g, w2_ref[lo:hi, :])
        acc = p if acc is None else acc + p
    return x + 0.5 * gate * _rms(acc, post_ref[...])


def _mod_kernel(c_ref, w_ref, b_ref, o_ref):
    c = c_ref[...]
    s = (c * jax.nn.sigmoid(c)).astype(BF16)
    o_ref[...] = _dot(s, w_ref[...].astype(BF16)) + b_ref[...]


def _mod_call(c, ada_w, ada_b):
    nb = c.shape[0]
    n = ada_w.shape[1]
    bn = D_MODEL
    return pl.pallas_call(
        _mod_kernel,
        grid=(n // bn,),
        in_specs=[
            pl.BlockSpec((nb, D_MODEL), lambda j: (0, 0)),
            pl.BlockSpec((D_MODEL, bn), lambda j: (0, j)),
            pl.BlockSpec((1, bn), lambda j: (0, j)),
        ],
        out_specs=pl.BlockSpec((nb, bn), lambda j: (0, j)),
        out_shape=jax.ShapeDtypeStruct((nb, n), F32),
        compiler_params=pltpu.CompilerParams(
            dimension_semantics=("arbitrary",), vmem_limit_bytes=VMEM_LIMIT),
        name="mod",
    )(c, ada_w, ada_b.reshape(1, n))


def _ffn_in_kernel(x_ref, mod_ref, pre_ref, post_ref, w1_ref, w3_ref, w2_ref,
                   nmpre_ref, win_ref, x1_ref, z_ref):
    x1 = _ffn(x_ref[0], mod_ref, 0, pre_ref, post_ref, w1_ref, w3_ref, w2_ref)
    x1_ref[0] = x1
    shift = mod_ref[0, 3:4, :]
    scale = mod_ref[0, 4:5, :]
    h = (_rms(x1, nmpre_ref[...]) * (1.0 + scale) + shift).astype(BF16)
    z_ref[0] = _dot(h, win_ref[...])


def _const_spec(shape):
    nd = len(shape)
    return pl.BlockSpec(shape, lambda *_: (0,) * nd, pipeline_mode=pl.Buffered(1))


def _ffn_in_call(x, mod, n_pre, n_post, w1, w3, w2, nm_pre, w_in):
    nb, t, d = x.shape
    f = w1.shape[1]
    tm = ROW_TILE
    row = lambda b, i: (b, i, 0)
    return pl.pallas_call(
        _ffn_in_kernel,
        grid=(nb, t // tm),
        in_specs=[
            pl.BlockSpec((1, tm, d), row),
            pl.BlockSpec((1, N_MOD, d), lambda b, i: (b, 0, 0)),
            _const_spec((1, d)), _const_spec((1, d)),
            _const_spec((d, f)), _const_spec((d, f)), _const_spec((f, d)),
            _const_spec((1, d)), _const_spec((d, Z_WIDTH)),
        ],
        out_specs=[pl.BlockSpec((1, tm, d), row), pl.BlockSpec((1, tm, Z_WIDTH), row)],
        out_shape=[jax.ShapeDtypeStruct((nb, t, d), F32),
                   jax.ShapeDtypeStruct((nb, t, Z_WIDTH), F32)],
        compiler_params=pltpu.CompilerParams(
            dimension_semantics=("parallel", "parallel"), vmem_limit_bytes=VMEM_LIMIT),
        name="ffn_in",
    )(x, mod, n_pre, n_post, w1, w3, w2, nm_pre, w_in)


def _bd(x, bdmask):
    return jnp.where(bdmask, jnp.concatenate([x, x], axis=0), 0.0)


def _wkv_kernel(reverse, with_bonus,
                z_ref, zp_ref, zn_ref, mu_ref, tri_ref, ones_ref,
                w0_ref, w2_ref, a0_ref, a2_ref, kk_ref, ka_ref, rk_ref, *rest):
    if with_bonus:
        a0b_ref, a2b_ref, y_ref, bonus_ref, h_ref = rest
    else:
        y_ref, h_ref = rest

    ti = pl.program_id(1)
    nt = pl.num_programs(1)
    nbb = z_ref.shape[0]

    @pl.when(ti == 0)
    def _():
        h_ref[...] = jnp.zeros_like(h_ref)

    tt = (nt - 1 - ti) if reverse else ti
    has_prev = (tt > 0).astype(F32)
    has_next = (tt < nt - 1).astype(F32)

    mu = mu_ref[...]
    zs = jnp.concatenate([
        _token_shift(z_ref[b], zp_ref[b, HALO - 1:HALO, :] * has_prev,
                     zn_ref[b, 0:1, :] * has_next, mu) for b in range(nbb)], axis=0)
    r = zs[:, 0:512]
    k = zs[:, 512:1024]
    v = zs[:, 1024:1536]
    zw = zs[:, 1536:1664]
    za = zs[:, 1664:1792]

    wl = w0_ref[...] + _dot(jnp.tanh(zw).astype(BF16), w2_ref[...])
    w_log = -_softplus(-wl) - 0.5
    lw = -jnp.exp(w_log)
    za16 = za.astype(BF16)
    a = jax.nn.sigmoid(a0_ref[...] + _dot(za16, a2_ref[...]))

    kk = k * kk_ref[...]
    kkn = kk / jnp.maximum(jnp.sqrt(_headsum(kk * kk, ones_ref)), L2_EPS)
    ka = ka_ref[...]
    k_d = k * (1.0 + (a - 1.0) * ka)
    b_ = kkn * a

    if with_bonus:
        a_o = jax.nn.sigmoid(a0b_ref[...] + _dot(za16, a2b_ref[...]))
        k_sum = k_d + k * (1.0 + (a_o - 1.0) * ka)
        bonus = _headsum(r * k_sum * rk_ref[...], ones_ref) * v
        for b in range(nbb):
            bonus_ref[b] = bonus[b * CHUNK:(b + 1) * CHUNK]

    tb = nbb * CHUNK
    tri = tri_ref[...]
    hi, mid, lo = _split3(lw)
    cc = _dot(tri, hi) + _dot(tri, mid) + _dot(tri, lo)
    c = cc[0:tb]
    ctot = cc[tb:2 * tb]
    at_all = (-kkn * jnp.exp(c - lw)).astype(BF16)
    rt_all = (r * jnp.exp(c)).astype(BF16)
    einv = jnp.exp(-c)
    kt_all = k_d * einv
    bt_all = b_ * einv
    erem = jnp.exp(ctot - c)
    kh = k_d * erem
    bh = b_ * erem
    gl_all = jnp.exp(ctot)

    lane128 = lax.broadcasted_iota(jnp.int32, (CHUNK, LANE), 1)
    row64 = lax.broadcasted_iota(jnp.int32, (CHUNK, LANE), 0)
    even_m = lane128 < HEAD_DIM
    ipair = jnp.where((lane128 % HEAD_DIM) == row64, 1.0, 0.0).astype(F32)
    r128 = lax.broadcasted_iota(jnp.int32, (LANE, LANE), 0)
    c128 = lax.broadcasted_iota(jnp.int32, (LANE, LANE), 1)
    bdmask = (r128 < HEAD_DIM) == (c128 < HEAD_DIM)
    eye128 = r128 == c128
    sr = lax.broadcasted_iota(jnp.int32, (2 * CHUNK, 2 * LANE), 0)
    sc = lax.broadcasted_iota(jnp.int32, (2 * CHUNK, 2 * LANE), 1)
    si = sr % CHUNK
    ss = sc % CHUNK
    incl = jnp.where(sr < CHUNK, 0, 1)
    trimask = (ss > si - incl) if reverse else (ss < si + incl)
    bdmask2 = jnp.concatenate([bdmask, bdmask], axis=1)
    zeros_pair = jnp.zeros((CHUNK, LANE), F32)
    zeros_bd = jnp.zeros((LANE, LANE), F32)

    chains = [(b, p) for b in range(nbb) for p in range(N_PAIRS)]

    def blk(x, b, p):
        return x[b * CHUNK:(b + 1) * CHUNK, p * LANE:(p + 1) * LANE]

    kbt_seq = []
    for b in range(nbb):
        rows = slice(b * CHUNK, (b + 1) * CHUNK)
        kbt_seq.append(jnp.concatenate([kh[rows], bh[rows]], axis=0).T.astype(BF16))

    at = [blk(at_all, b, p) for b, p in chains]
    rt = [blk(rt_all, b, p) for b, p in chains]
    vv = [blk(v, b, p) for b, p in chains]
    bdv = [_bd(x, bdmask) for x in vv]

    s = []
    for i, (b, p) in enumerate(chains):
        ktf = blk(kt_all, b, p)
        btf = blk(bt_all, b, p)
        lhs1 = jnp.concatenate([at[i], rt[i]], axis=0)
        rhsm = jnp.concatenate([
            jnp.where(even_m, ktf, 0.0), jnp.where(even_m, 0.0, ktf),
            jnp.where(even_m, btf, 0.0), jnp.where(even_m, 0.0, btf)], axis=0).astype(BF16)
        s.append(jnp.where(trimask, _dot_nt(lhs1, rhsm), 0.0))
    ak = [x[0:CHUNK, 0:LANE] for x in s]
    n1 = [x[0:CHUNK, LANE:2 * LANE] for x in s]
    rkrb = [x[CHUNK:2 * CHUNK, :].astype(BF16) for x in s]

    g = [_dot(a_.astype(BF16), m.astype(BF16)) for a_, m in zip(ak, bdv)]

    cur = [_dot(n.astype(BF16), _bd(n, bdmask).astype(BF16)) for n in n1]
    x = [ipair + n for n in n1]
    for lvl in range(5):
        rhs = [_bd(n, bdmask).astype(BF16) for n in cur]
        if lvl < 4:
            out = [_dot(jnp.concatenate([xi, ni], axis=0).astype(BF16), m)
                   for xi, ni, m in zip(x, cur, rhs)]
            x = [xi + o[0:CHUNK] for xi, o in zip(x, out)]
            cur = [o[CHUNK:] for o in out]
        else:
            x = [xi + _dot(xi.astype(BF16), m) for xi, m in zip(x, rhs)]

    au = [_dot(xi.astype(BF16), jnp.concatenate(
        [_bd(a_.astype(F32), bdmask), _bd(gi, bdmask)], axis=1).astype(BF16))
        for xi, a_, gi in zip(x, at, g)]
    a1 = [o[:, 0:LANE] for o in au]
    u0 = [o[:, LANE:2 * LANE] for o in au]

    ry = []
    pq = []
    for i, (b, p) in enumerate(chains):
        rhs2 = jnp.concatenate([
            jnp.concatenate([zeros_bd, bdv[i]], axis=1),
            jnp.concatenate([_bd(a1[i], bdmask), _bd(u0[i], bdmask)], axis=1)], axis=0)
        ry.append(_dot(rkrb[i], rhs2.astype(BF16)))
    for i, (b, p) in enumerate(chains):
        rhs3 = jnp.concatenate([
            jnp.concatenate([zeros_pair, vv[i]], axis=1),
            jnp.concatenate([a1[i], u0[i]], axis=1)], axis=0)
        kbt = kbt_seq[b][p * LANE:(p + 1) * LANE, :]
        pq.append(jnp.where(bdmask2, _dot(kbt, rhs3.astype(BF16)), 0.0))

    for i, (b, p) in enumerate(chains):
        lanes = slice(p * LANE, (p + 1) * LANE)
        r1 = rt[i].astype(F32) + ry[i][:, 0:LANE]
        y0 = ry[i][:, LANE:2 * LANE]
        gl = gl_all[b * CHUNK:b * CHUNK + 1, lanes]
        pbd = pq[i][:, 0:LANE] + jnp.where(eye128, gl, 0.0)
        qbd = pq[i][:, LANE:2 * LANE]
        h = h_ref[i]
        zz = _dot(jnp.concatenate([r1, pbd], axis=0).astype(BF16), h.astype(BF16))
        y_ref[b, :, lanes] = zz[0:CHUNK] + y0
        h_ref[i] = zz[CHUNK:] + qbd


def _wkv_call(reverse, z, mu_r, tri, ones, w0, w2p, a0, a2p, k_k, k_a, r_k, bonus_args=None):
    nb, t, _ = z.shape
    bb = WKV_BATCH
    nt = t // CHUNK
    nhalo = CHUNK // HALO
    last_halo = t // HALO - 1
    with_bonus = bonus_args is not None

    def tsel(i):
        return (nt - 1 - i) if reverse else i

    main = lambda g, i: (g, tsel(i), 0)
    prev = lambda g, i: (g, jnp.maximum(tsel(i) * nhalo - 1, 0), 0)
    nxt = lambda g, i: (g, jnp.minimum((tsel(i) + 1) * nhalo, last_halo), 0)

    in_specs = [
        pl.BlockSpec((bb, CHUNK, Z_RWKV), main),
        pl.BlockSpec((bb, HALO, Z_RWKV), prev),
        pl.BlockSpec((bb, HALO, Z_RWKV), nxt),
        _const_spec((1, Z_RWKV)),
        _const_spec((2 * bb * CHUNK, bb * CHUNK)),
        _const_spec((D_RWKV, D_RWKV)),
        _const_spec((1, D_RWKV)), _const_spec((LANE, D_RWKV)),
        _const_spec((1, D_RWKV)), _const_spec((LANE, D_RWKV)),
        _const_spec((1, D_RWKV)), _const_spec((1, D_RWKV)), _const_spec((1, D_RWKV)),
    ]
    args = [z, z, z, mu_r, tri, ones, w0, w2p, a0, a2p, k_k, k_a, r_k]
    out_specs = [pl.BlockSpec((bb, CHUNK, D_RWKV), main)]
    out_shape = [jax.ShapeDtypeStruct((nb, t, D_RWKV), F32)]
    if with_bonus:
        in_specs += [_const_spec((1, D_RWKV)), _const_spec((LANE, D_RWKV))]
        args += list(bonus_args)
        out_specs.append(pl.BlockSpec((bb, CHUNK, D_RWKV), main))
        out_shape.append(jax.ShapeDtypeStruct((nb, t, D_RWKV), F32))

    return pl.pallas_call(
        functools.partial(_wkv_kernel, reverse, with_bonus),
        grid=(nb // bb, nt),
        in_specs=in_specs,
        out_specs=out_specs,
        out_shape=out_shape,
        scratch_shapes=[pltpu.VMEM((bb * N_PAIRS, LANE, LANE), F32)],
        compiler_params=pltpu.CompilerParams(
            dimension_semantics=("parallel", "arbitrary"), vmem_limit_bytes=VMEM_LIMIT),
        name="wkv_bwd" if reverse else "wkv_fwd",
    )(*args)


def _mix_out_kernel(x1_ref, mod_ref, y0_ref, y1_ref, bonus_ref,
                    zg_ref, zgp_ref, zgn_ref, zq_ref, zqp_ref, zqn_ref,
                    mug_ref, ones_ref, g2_ref, lng_ref, lnb_ref, pw_ref, ps_ref,
                    wout_ref, nmpost_ref, pre_ref, post_ref, w1_ref, w3_ref, w2_ref, o_ref):
    ti = pl.program_id(1)
    nt = pl.num_programs(1)
    tm = x1_ref.shape[1]
    has_prev = (ti > 0).astype(F32)
    has_next = (ti < nt - 1).astype(F32)

    y = y0_ref[0] + y1_ref[0]
    inv_n = 1.0 / HEAD_DIM
    mean = _headsum(y, ones_ref) * inv_n
    dlt = y - mean
    var = _headsum(dlt * dlt, ones_ref) * inv_n
    yn = dlt * lax.rsqrt(var + GN_EPS) * lng_ref[...] + lnb_ref[...]
    zg = _token_shift(zg_ref[0], zgp_ref[0, HALO - 1:HALO, :] * has_prev,
                      zgn_ref[0, 0:1, :] * has_next, mug_ref[...])
    gate = _dot(jax.nn.sigmoid(zg).astype(BF16), g2_ref[...])
    m_r = ((yn + bonus_ref[0]) * gate).astype(BF16)

    ext = jnp.concatenate([zqp_ref[0] * has_prev, zq_ref[0], zqn_ref[0] * has_next], axis=0)
    n = ext.shape[0]
    tglob = ti * tm + lax.broadcasted_iota(jnp.int32, (tm, POOL_GROUP), 0)
    t_total = nt * tm
    cur = ext
    m_out = _dot(m_r, wout_ref[D_POOL:D_MODEL, :])
    for gi, win in enumerate(POOL_WINDOWS):
        if gi == 0:
            cur = cur + pltpu.roll(cur, 1, 0)
        else:
            sh = win // 4
            cur = cur[:, POOL_GROUP:]
            cur = pltpu.roll(cur, n - sh, 0) + pltpu.roll(cur, sh, 0)
        wsum = cur[HALO:HALO + tm, 0:POOL_GROUP]
        half = win // 2
        cnt = (jnp.minimum(tglob + half, t_total) - jnp.maximum(tglob - half, 0)).astype(F32)
        zc = ext[HALO:HALO + tm, gi * POOL_GROUP:(gi + 1) * POOL_GROUP]
        pg = (wsum / cnt - zc).astype(BF16)
        po = _dot(pg, pw_ref[gi]) * ps_ref[:, gi * POOL_GROUP:(gi + 1) * POOL_GROUP]
        m_out = m_out + _dot(po.astype(BF16), wout_ref[gi * POOL_GROUP:(gi + 1) * POOL_GROUP, :])

    x2 = x1_ref[0] + mod_ref[0, 5:6, :] * _rms(m_out, nmpost_ref[...])
    o_ref[0] = _ffn(x2, mod_ref, 6, pre_ref, post_ref, w1_ref, w3_ref, w2_ref)


def _mix_out_call(x1, mod, y0, y1, bonus, z, mu_g, ones, g2p, lnx_g, lnx_b, pool_w, pool_scale,
                  w_out, nm_post, n_pre, n_post, w1, w3, w2):
    nb, t, d = x1.shape
    f = w1.shape[1]
    tm = ROW_TILE
    nhalo = tm // HALO
    last_halo = t // HALO - 1
    gw = 2 * LANE
    gblk = Z_RWKV // gw - 1
    qblk = Z_RWKV // D_POOL
    row = lambda b, i: (b, i, 0)

    def col(blk):
        return (lambda b, i: (b, i, blk),
                lambda b, i: (b, jnp.maximum(i * nhalo - 1, 0), blk),
                lambda b, i: (b, jnp.minimum((i + 1) * nhalo, last_halo), blk))

    g_main, g_prev, g_next = col(gblk)
    q_main, q_prev, q_next = col(qblk)
    return pl.pallas_call(
        _mix_out_kernel,
        grid=(nb, t // tm),
        in_specs=[
            pl.BlockSpec((1, tm, d), row),
            pl.BlockSpec((1, N_MOD, d), lambda b, i: (b, 0, 0)),
            pl.BlockSpec((1, tm, D_RWKV), row),
            pl.BlockSpec((1, tm, D_RWKV), row),
            pl.BlockSpec((1, tm, D_RWKV), row),
            pl.BlockSpec((1, tm, gw), g_main),
            pl.BlockSpec((1, HALO, gw), g_prev),
            pl.BlockSpec((1, HALO, gw), g_next),
            pl.BlockSpec((1, tm, D_POOL), q_main),
            pl.BlockSpec((1, HALO, D_POOL), q_prev),
            pl.BlockSpec((1, HALO, D_POOL), q_next),
            _const_spec((1, gw)), _const_spec((D_RWKV, D_RWKV)), _const_spec((gw, D_RWKV)),
            _const_spec((1, D_RWKV)), _const_spec((1, D_RWKV)),
            _const_spec((len(POOL_WINDOWS), POOL_GROUP, POOL_GROUP)), _const_spec((1, D_POOL)),
            _const_spec((d, d)), _const_spec((1, d)),
            _const_spec((1, d)), _const_spec((1, d)),
            _const_spec((d, f)), _const_spec((d, f)), _const_spec((f, d)),
        ],
        out_specs=pl.BlockSpec((1, tm, d), row),
        out_shape=jax.ShapeDtypeStruct((nb, t, d), F32),
        compiler_params=pltpu.CompilerParams(
            dimension_semantics=("parallel", "parallel"), vmem_limit_bytes=VMEM_LIMIT),
        name="mix_out",
    )(x1, mod, y0, y1, bonus, z, z, z, z, z, z, mu_g, ones, g2p, lnx_g, lnx_b, pool_w, pool_scale,
      w_out, nm_post, n_pre, n_post, w1, w3, w2)


def _tri_consts(tb):
    i = np.arange(tb)[:, None]
    j = np.arange(tb)[None, :]
    same = (i // CHUNK) == (j // CHUNK)
    fwd = np.concatenate([same & (j <= i), same], axis=0)
    bwd = np.concatenate([same & (j >= i), same], axis=0)
    return jnp.asarray(fwd, BF16), jnp.asarray(bwd, BF16)


def _head_ones():
    i = np.arange(D_RWKV)
    return jnp.asarray((i[:, None] // HEAD_DIM) == (i[None, :] // HEAD_DIM), BF16)


def _layer(x, c_mod, p):
    x1, z = _ffn_in_call(x, c_mod, p["n1_pre"], p["n1_post"], p["f1_w1"], p["f1_w3"], p["f1_w2"],
                         p["nm_pre"], p["w_in"])
    y0, bonus = _wkv_call(False, z, p["mu_r"], p["tri_f"], p["ones"], p["w0"][0], p["w2p"][0],
                          p["a0"][0], p["a2p"][0], p["k_k"], p["k_a"], p["r_k"],
                          bonus_args=(p["a0"][1], p["a2p"][1]))
    (y1,) = _wkv_call(True, z, p["mu_r"], p["tri_b"], p["ones"], p["w0"][1], p["w2p"][1],
                      p["a0"][1], p["a2p"][1], p["k_k"], p["k_a"], p["r_k"])
    return _mix_out_call(x1, c_mod, y0, y1, bonus, z, p["mu_g"], p["ones"], p["g2p"],
                         p["lnx_g"], p["lnx_b"], p["pool_w"], p["pool_scale"], p["w_out"],
                         p["nm_post"], p["n2_pre"], p["n2_post"], p["f2_w1"], p["f2_w3"], p["f2_w2"])


def _prepare(i, n1_pre, n1_post, f1_w1, f1_w3, f1_w2, nm_pre, nm_post, w_in, shift_mu, pool_w,
             pool_scale, w0, w2, a0, a2, g2, k_k, k_a, r_k, lnx_g, lnx_b, w_out,
             n2_pre, n2_post, f2_w1, f2_w3, f2_w2):
    row = lambda a: a[i].reshape(1, -1)
    n_real = w_in.shape[2] - D_POOL
    pad = Z_RWKV - n_real
    w_in_i = w_in[i]
    win = jnp.concatenate(
        [w_in_i[:, D_POOL:], jnp.zeros((D_MODEL, pad), F32), w_in_i[:, :D_POOL]], axis=1)
    mu_r = jnp.concatenate([shift_mu[i], jnp.zeros((pad,), F32)]).reshape(1, Z_RWKV)
    half = LANE // 2
    zpad = jnp.zeros((half, D_RWKV), F32)
    w2p = [jnp.concatenate([w2[i, 0], zpad], 0), jnp.concatenate([zpad, w2[i, 1]], 0)]
    a2p = [jnp.concatenate([a2[i, 0], zpad], 0), jnp.concatenate([zpad, a2[i, 1]], 0)]
    g2p = jnp.concatenate([g2[i], jnp.zeros((2 * LANE - R_GATE, D_RWKV), F32)], 0)
    tri_f, tri_b = _tri_consts(WKV_BATCH * CHUNK)
    return dict(
        n1_pre=row(n1_pre), n1_post=row(n1_post),
        f1_w1=f1_w1[i].astype(BF16), f1_w3=f1_w3[i].astype(BF16), f1_w2=f1_w2[i].astype(BF16),
        nm_pre=row(nm_pre), nm_post=row(nm_post), w_in=win.astype(BF16),
        mu_r=mu_r, mu_g=mu_r[:, Z_RWKV - 2 * LANE:],
        pool_w=pool_w[i].astype(BF16), pool_scale=row(pool_scale),
        w0=[w0[i, d].reshape(1, -1) for d in range(2)], w2p=[m.astype(BF16) for m in w2p],
        a0=[a0[i, d].reshape(1, -1) for d in range(2)], a2p=[m.astype(BF16) for m in a2p],
        g2p=g2p.astype(BF16), k_k=row(k_k), k_a=row(k_a), r_k=row(r_k),
        lnx_g=row(lnx_g), lnx_b=row(lnx_b), w_out=w_out[i].astype(BF16),
        n2_pre=row(n2_pre), n2_post=row(n2_post),
        f2_w1=f2_w1[i].astype(BF16), f2_w3=f2_w3[i].astype(BF16), f2_w2=f2_w2[i].astype(BF16),
        tri_f=tri_f, tri_b=tri_b, ones=_head_ones(),
    )


def kernel(x_prompt, x_sample, c_prompt, c_sample, ada_w, ada_b, n1_pre, n1_post, f1_w1, f1_w3, f1_w2, nm_pre, nm_post, w_in, shift_mu, pool_w, pool_scale, w0, w2, a0, a2, g2, k_k, k_a, r_k, lnx_g, lnx_b, w_out, n2_pre, n2_post, f2_w1, f2_w3, f2_w2):
    depth = ada_w.shape[0]
    nbp = x_prompt.shape[0]
    xs = [x_prompt, x_sample]
    c_all = jnp.concatenate([c_prompt, c_sample], axis=0)
    for i in range(depth):
        p = _prepare(i, n1_pre, n1_post, f1_w1, f1_w3, f1_w2, nm_pre, nm_post, w_in, shift_mu,
                     pool_w, pool_scale, w0, w2, a0, a2, g2, k_k, k_a, r_k, lnx_g, lnx_b, w_out,
                     n2_pre, n2_post, f2_w1, f2_w3, f2_w2)
        mod = _mod_call(c_all, ada_w[i], ada_b[i]).reshape(c_all.shape[0], N_MOD, D_MODEL)
        mods = [mod[:nbp], mod[nbp:]]
        xs = [_layer(x, m, p) for x, m in zip(xs, mods)]
    return (xs[0], xs[1])
```

```python
import functools

import numpy as np
import jax
import jax.numpy as jnp
from jax import lax
from jax.experimental import pallas as pl
from jax.experimental.pallas import tpu as pltpu

F32 = jnp.float32
BF16 = jnp.bfloat16

D_MODEL = 1024
D_POOL = 512
D_RWKV = 512
HEAD_DIM = 64
N_PAIRS = D_RWKV // (2 * HEAD_DIM)
N_DIR = 2
POOL_WINDOWS = (2, 4, 8, 16)
POOL_GROUP = 128
R_GATE = 160
N_MOD = 9
RMS_EPS = 1e-6
GN_EPS = 64e-5
L2_EPS = 1e-12

Z_RWKV = 2048
Z_WIDTH = Z_RWKV + D_POOL
Z_SHIFT = 1792
LANE = 128
SUBLANE = 8
CHUNK = 64
HALO = SUBLANE

ROW_TILE = 256
SCAN_BATCH = 8
F_CHUNKS = ((0, 768), (768, 1536), (1536, 2304), (2304, 2816))
VMEM_LIMIT = 56 * 1024 * 1024


def _rms(x, g):
    ms = jnp.mean(x * x, axis=-1, keepdims=True)
    return x * lax.rsqrt(ms + RMS_EPS) * g


def _dot(a, b):
    return jnp.dot(a, b, preferred_element_type=F32)


def _dot_nt(a, b):
    return lax.dot_general(a, b, (((1,), (1,)), ((), ())), preferred_element_type=F32)


def _split2(x):
    hi = x.astype(BF16)
    lo = (x - hi.astype(F32)).astype(BF16)
    return hi, lo


def _headsum(x, ones_ref):
    hi, lo = _split2(x)
    ones = ones_ref[...]
    return _dot(hi, ones) + _dot(lo, ones)


def _softplus(x):
    return jnp.maximum(x, 0.0) + jnp.log(1.0 + jnp.exp(-jnp.abs(x)))


def _token_shift(z, before, after, mu):
    n = z.shape[0]
    ext = jnp.concatenate([before, z, after], axis=0)
    m = ext.shape[0]
    prev = pltpu.roll(ext, 1, 0)[HALO:HALO + n]
    nxt = pltpu.roll(ext, m - 1, 0)[HALO:HALO + n]
    return z * (1.0 - mu) + (prev + nxt) * (0.5 * mu)


def _ffn(x, mod_ref, j, pre_ref, post_ref, w1_ref, w3_ref, w2_ref):
    shift = mod_ref[0, j:j + 1, :]
    scale = mod_ref[0, j + 1:j + 2, :]
    gate = mod_ref[0, j + 2:j + 3, :]
    h = (_rms(x, pre_ref[...]) * (1.0 + scale) + shift).astype(BF16)
    acc = None
    for lo, hi in F_CHUNKS:
        a = _dot(h, w1_ref[:, lo:hi])
        b = _dot(h, w3_ref[:, lo:hi])
        g = (a * jax.nn.sigmoid(a) * b).astype(BF16)
        p = _dot(g, w2_ref[lo:hi, :])
        acc = p if acc is None else acc + p
    return x + 0.5 * gate * _rms(acc, post_ref[...])


def _const_spec(shape):
    nd = len(shape)
    return pl.BlockSpec(shape, lambda *_: (0,) * nd, pipeline_mode=pl.Buffered(1))


def _halo_maps(tile_rows, total_rows, col=0):
    per = tile_rows // HALO
    last = total_rows // HALO - 1
    return (lambda b, i: (b, i, col),
            lambda b, i: (b, jnp.maximum(i * per - 1, 0), col),
            lambda b, i: (b, jnp.minimum((i + 1) * per, last), col))


def _mod_kernel(c_ref, w_ref, b_ref, o_ref):
    c = c_ref[...]
    s = (c * jax.nn.sigmoid(c)).astype(BF16)
    o_ref[...] = _dot(s, w_ref[...].astype(BF16)) + b_ref[...]


def _mod_call(c, ada_w, ada_b):
    nb = c.shape[0]
    n = ada_w.shape[1]
    bn = D_MODEL
    return pl.pallas_call(
        _mod_kernel,
        grid=(n // bn,),
        in_specs=[
            pl.BlockSpec((nb, D_MODEL), lambda j: (0, 0)),
            pl.BlockSpec((D_MODEL, bn), lambda j: (0, j)),
            pl.BlockSpec((1, bn), lambda j: (0, j)),
        ],
        out_specs=pl.BlockSpec((nb, bn), lambda j: (0, j)),
        out_shape=jax.ShapeDtypeStruct((nb, n), F32),
        compiler_params=pltpu.CompilerParams(
            dimension_semantics=("arbitrary",), vmem_limit_bytes=VMEM_LIMIT),
        name="mod",
    )(c, ada_w, ada_b.reshape(1, n))


def _ffn_in_kernel(x_ref, mod_ref, pre_ref, post_ref, w1_ref, w3_ref, w2_ref,
                   nmpre_ref, win_ref, x1_ref, z_ref):
    x1 = _ffn(x_ref[0], mod_ref, 0, pre_ref, post_ref, w1_ref, w3_ref, w2_ref)
    x1_ref[0] = x1
    shift = mod_ref[0, 3:4, :]
    scale = mod_ref[0, 4:5, :]
    h = (_rms(x1, nmpre_ref[...]) * (1.0 + scale) + shift).astype(BF16)
    z_ref[0] = _dot(h, win_ref[...])


def _ffn_in_call(x, mod, n_pre, n_post, w1, w3, w2, nm_pre, w_in):
    nb, t, d = x.shape
    f = w1.shape[1]
    tm = ROW_TILE
    row = lambda b, i: (b, i, 0)
    return pl.pallas_call(
        _ffn_in_kernel,
        grid=(nb, t // tm),
        in_specs=[
            pl.BlockSpec((1, tm, d), row),
            pl.BlockSpec((1, N_MOD, d), lambda b, i: (b, 0, 0)),
            _const_spec((1, d)), _const_spec((1, d)),
            _const_spec((d, f)), _const_spec((d, f)), _const_spec((f, d)),
            _const_spec((1, d)), _const_spec((d, Z_WIDTH)),
        ],
        out_specs=[pl.BlockSpec((1, tm, d), row), pl.BlockSpec((1, tm, Z_WIDTH), row)],
        out_shape=[jax.ShapeDtypeStruct((nb, t, d), F32),
                   jax.ShapeDtypeStruct((nb, t, Z_WIDTH), F32)],
        compiler_params=pltpu.CompilerParams(
            dimension_semantics=("parallel", "parallel"), vmem_limit_bytes=VMEM_LIMIT),
        name="ffn_in",
    )(x, mod, n_pre, n_post, w1, w3, w2, nm_pre, w_in)


def _prep_kernel(z_ref, zp_ref, zn_ref, mu_ref, trif_ref, trib_ref, ones_ref,
                 w0_ref, w2_ref, a0_ref, a2_ref, kk_ref, ka_ref, rk_ref,
                 v_ref, bonus_ref, *outs):
    ti = pl.program_id(1)
    nt = pl.num_programs(1)
    tm = z_ref.shape[1]
    nc = tm // CHUNK
    has_prev = (ti > 0).astype(F32)
    has_next = (ti < nt - 1).astype(F32)

    zs = _token_shift(z_ref[0], zp_ref[0] * has_prev, zn_ref[0] * has_next, mu_ref[...])
    r = zs[:, 0:512]
    k = zs[:, 512:1024]
    v = zs[:, 1024:1536]
    x2 = w0_ref[...] + _dot(jnp.tanh(zs[:, 1536:1664]).astype(BF16), w2_ref[...])
    am2 = a0_ref[...] + _dot(zs[:, 1664:1792].astype(BF16), a2_ref[...])
    kk = k * kk_ref[...]
    kkn = kk * lax.rsqrt(jnp.maximum(_dot((kk * kk).astype(BF16), ones_ref[...]), L2_EPS * L2_EPS))
    nkkn = -kkn
    kka = k * ka_ref[...]
    kbase = k - kka
    a = [1.0 / (1.0 + jnp.exp(-am2[:, d * D_RWKV:(d + 1) * D_RWKV])) for d in range(N_DIR)]
    k_d = [kbase + kka * a[d] for d in range(N_DIR)]

    v_ref[0] = v.astype(BF16)
    bonus_ref[0] = _dot((r * (k_d[0] + k_d[1]) * rk_ref[...]).astype(BF16), ones_ref[...]) * v

    for d in range(N_DIR):
        at_ref, rt_ref, kt_ref, bt_ref, kbt_ref, gl_ref = outs[6 * d:6 * d + 6]
        x = x2[:, d * D_RWKV:(d + 1) * D_RWKV]
        w_log = jnp.minimum(x, 0.0) - jnp.log(1.0 + jnp.exp(-jnp.abs(x))) - 0.5
        nlw = jnp.exp(w_log)
        hi, lo = _split2(nlw)
        tri = (trib_ref if d else trif_ref)[...]
        nc_ = _dot(tri, hi) + _dot(tri, lo)
        einv = jnp.exp(nc_)
        b_ = kkn * a[d]
        kt = k_d[d] * einv
        bt = b_ * einv
        at_ref[0] = (nkkn * jnp.exp(nlw - nc_)).astype(BF16)
        rt_ref[0] = (r * jnp.exp(-nc_)).astype(BF16)
        kt_ref[0] = kt.astype(BF16)
        bt_ref[0] = bt.astype(BF16)
        end = 0 if d else CHUNK - 1
        for j in range(nc):
            rows = slice(j * CHUNK, (j + 1) * CHUNK)
            gl = jnp.exp(-nc_[j * CHUNK + end:j * CHUNK + end + 1])
            gl_ref[0, j] = jnp.broadcast_to(gl, (SUBLANE, D_RWKV))
            kbt = jnp.concatenate([kt[rows] * gl, bt[rows] * gl], axis=0).T.astype(BF16)
            kbt_ref[0, j] = jnp.concatenate(
                [kbt[h * HEAD_DIM:(h + 1) * HEAD_DIM] for h in range(D_RWKV // HEAD_DIM)], axis=1)


def _prep_call(z, p):
    nb, t, _ = z.shape
    tm = ROW_TILE
    nc = tm // CHUNK
    main, prev, nxt = _halo_maps(tm, t)
    row = lambda b, i: (b, i, 0)
    chunked = lambda b, i: (b, i, 0, 0)
    wide = N_DIR * D_RWKV
    tok = lambda dt: (pl.BlockSpec((1, tm, D_RWKV), row), jax.ShapeDtypeStruct((nb, t, D_RWKV), dt))
    per_dir = [tok(BF16), tok(BF16), tok(BF16), tok(BF16),
               (pl.BlockSpec((1, nc, HEAD_DIM, 2 * D_RWKV), chunked),
                jax.ShapeDtypeStruct((nb, t // CHUNK, HEAD_DIM, 2 * D_RWKV), BF16)),
               (pl.BlockSpec((1, nc, SUBLANE, D_RWKV), chunked),
                jax.ShapeDtypeStruct((nb, t // CHUNK, SUBLANE, D_RWKV), F32))]
    outs = [tok(BF16), tok(F32)] + per_dir * N_DIR
    res = pl.pallas_call(
        _prep_kernel,
        grid=(nb, t // tm),
        in_specs=[
            pl.BlockSpec((1, tm, Z_SHIFT), main),
            pl.BlockSpec((1, HALO, Z_SHIFT), prev),
            pl.BlockSpec((1, HALO, Z_SHIFT), nxt),
            _const_spec((1, Z_SHIFT)),
            _const_spec((tm, tm)), _const_spec((tm, tm)),
            _const_spec((D_RWKV, D_RWKV)),
            _const_spec((1, wide)), _const_spec((LANE, wide)),
            _const_spec((1, wide)), _const_spec((LANE, wide)),
            _const_spec((1, D_RWKV)), _const_spec((1, D_RWKV)), _const_spec((1, D_RWKV)),
        ],
        out_specs=[o[0] for o in outs],
        out_shape=[o[1] for o in outs],
        compiler_params=pltpu.CompilerParams(
            dimension_semantics=("parallel", "parallel"), vmem_limit_bytes=VMEM_LIMIT),
        name="wkv_prep",
    )(z, z, z, p["mu_r"][:, :Z_SHIFT], p["tri_f"], p["tri_b"], p["ones"],
      p["w0"], p["w2p"], p["a0"], p["a2p"], p["k_k"], p["k_a"], p["r_k"])
    v16, bonus = res[0], res[1]
    return v16, bonus, [res[2 + 6 * d:8 + 6 * d] for d in range(N_DIR)]


def _bd(x, bdmask):
    return jnp.where(bdmask, jnp.concatenate([x, x], axis=0), 0.0)


def _scan_kernel(reverse, at_ref, rt_ref, kt_ref, bt_ref, v_ref, kbt_ref, gl_ref, y_ref, h_ref):
    nbb = at_ref.shape[0]

    @pl.when(pl.program_id(1) == 0)
    def _():
        h_ref[...] = jnp.zeros_like(h_ref)

    lane128 = lax.broadcasted_iota(jnp.int32, (CHUNK, LANE), 1)
    row64 = lax.broadcasted_iota(jnp.int32, (CHUNK, LANE), 0)
    even_m = lane128 < HEAD_DIM
    ipair = jnp.where((lane128 % HEAD_DIM) == row64, 1.0, 0.0).astype(F32)
    r128 = lax.broadcasted_iota(jnp.int32, (LANE, LANE), 0)
    c128 = lax.broadcasted_iota(jnp.int32, (LANE, LANE), 1)
    bdmask = (r128 < HEAD_DIM) == (c128 < HEAD_DIM)
    sr = lax.broadcasted_iota(jnp.int32, (2 * CHUNK, 2 * LANE), 0)
    sc = lax.broadcasted_iota(jnp.int32, (2 * CHUNK, 2 * LANE), 1)
    si = sr % CHUNK
    ss = sc % CHUNK
    incl = jnp.where(sr < CHUNK, 0, 1)
    trimask = (ss > si - incl) if reverse else (ss < si + incl)
    zeros_bd = jnp.zeros((LANE, LANE), F32)

    chains = [(b, p) for b in range(nbb) for p in range(N_PAIRS)]

    def blk(ref, b, p):
        return ref[b, :, p * LANE:(p + 1) * LANE]

    at = [blk(at_ref, b, p) for b, p in chains]
    rt = [blk(rt_ref, b, p) for b, p in chains]
    vv = [blk(v_ref, b, p).astype(F32) for b, p in chains]
    bdv = [_bd(x, bdmask) for x in vv]

    s = []
    for i, (b, p) in enumerate(chains):
        ktf = blk(kt_ref, b, p).astype(F32)
        btf = blk(bt_ref, b, p).astype(F32)
        lhs1 = jnp.concatenate([at[i], rt[i]], axis=0)
        rhsm = jnp.concatenate([
            jnp.where(even_m, ktf, 0.0), jnp.where(even_m, 0.0, ktf),
            jnp.where(even_m, btf, 0.0), jnp.where(even_m, 0.0, btf)], axis=0).astype(BF16)
        s.append(jnp.where(trimask, _dot_nt(lhs1, rhsm), 0.0))
    ak = [x[0:CHUNK, 0:LANE] for x in s]
    n1 = [x[0:CHUNK, LANE:2 * LANE] for x in s]
    rkrb = [x[CHUNK:2 * CHUNK, :].astype(BF16) for x in s]

    g = [_dot(a_.astype(BF16), m.astype(BF16)) for a_, m in zip(ak, bdv)]

    cur = [_dot(n.astype(BF16), _bd(n, bdmask).astype(BF16)) for n in n1]
    x = [ipair + n for n in n1]
    for lvl in range(5):
        rhs = [_bd(n, bdmask).astype(BF16) for n in cur]
        if lvl < 4:
            out = [_dot(jnp.concatenate([xi, ni], axis=0).astype(BF16), m)
                   for xi, ni, m in zip(x, cur, rhs)]
            x = [xi + o[0:CHUNK] for xi, o in zip(x, out)]
            cur = [o[CHUNK:] for o in out]
        else:
            x = [xi + _dot(xi.astype(BF16), m) for xi, m in zip(x, rhs)]

    au = [_dot(xi.astype(BF16), jnp.concatenate(
        [_bd(a_.astype(F32), bdmask), _bd(gi, bdmask)], axis=1).astype(BF16))
        for xi, a_, gi in zip(x, at, g)]
    a1 = [o[:, 0:LANE] for o in au]
    u0 = [o[:, LANE:2 * LANE] for o in au]

    ry = []
    pq = []
    top = []
    bot = []
    for i in range(len(chains)):
        top.append(jnp.concatenate([zeros_bd, bdv[i]], axis=1).astype(BF16))
        bot.append(jnp.concatenate([_bd(a1[i], bdmask), _bd(u0[i], bdmask)], axis=1).astype(BF16))
        ry.append(_dot(rkrb[i], jnp.concatenate([top[i], bot[i]], axis=0)))
    for i, (b, p) in enumerate(chains):
        rhs3 = jnp.concatenate([top[i][0:CHUNK], bot[i][0:CHUNK], top[i][CHUNK:], bot[i][CHUNK:]], axis=0)
        kbt = kbt_ref[b, 0, :, 2 * p * LANE:2 * (p + 1) * LANE]
        pq.append(_dot(kbt, rhs3))

    for i, (b, p) in enumerate(chains):
        lanes = slice(p * LANE, (p + 1) * LANE)
        r1 = rt[i].astype(F32) + ry[i][:, 0:LANE]
        y0 = ry[i][:, LANE:2 * LANE]
        gl = gl_ref[b, 0, 0:1, lanes]
        pp = pq[i][:, 0:LANE] + ipair * gl
        qq = pq[i][:, LANE:2 * LANE]
        zz = _dot(jnp.concatenate([r1, pp], axis=0).astype(BF16), _bd(h_ref[i], bdmask).astype(BF16))
        y_ref[b, :, lanes] = zz[0:CHUNK] + y0
        h_ref[i] = zz[CHUNK:] + qq


def _scan_call(reverse, v16, ops):
    at, rt, kt, bt, kbt, gl = ops
    nb, t, _ = at.shape
    bb = SCAN_BATCH
    nt = t // CHUNK
    tsel = (lambda i: nt - 1 - i) if reverse else (lambda i: i)
    tok_map = lambda g, i: (g, tsel(i), 0)
    chunk_map = lambda g, i: (g, tsel(i), 0, 0)
    tok = pl.BlockSpec((bb, CHUNK, D_RWKV), tok_map)
    return pl.pallas_call(
        functools.partial(_scan_kernel, reverse),
        grid=(nb // bb, nt),
        in_specs=[tok, tok, tok, tok, tok,
                  pl.BlockSpec((bb, 1, HEAD_DIM, 2 * D_RWKV), chunk_map),
                  pl.BlockSpec((bb, 1, SUBLANE, D_RWKV), chunk_map)],
        out_specs=tok,
        out_shape=jax.ShapeDtypeStruct((nb, t, D_RWKV), F32),
        scratch_shapes=[pltpu.VMEM((bb * N_PAIRS, CHUNK, LANE), F32)],
        compiler_params=pltpu.CompilerParams(
            dimension_semantics=("parallel", "arbitrary"), vmem_limit_bytes=VMEM_LIMIT),
        name="wkv_scan_bwd" if reverse else "wkv_scan_fwd",
    )(at, rt, kt, bt, v16, kbt, gl)


def _mix_out_kernel(x1_ref, mod_ref, y0_ref, y1_ref, bonus_ref,
                    zg_ref, zgp_ref, zgn_ref, zq_ref, zqp_ref, zqn_ref,
                    mug_ref, ones_ref, g2_ref, lng_ref, lnb_ref, pw_ref, ps_ref,
                    wout_ref, nmpost_ref, pre_ref, post_ref, w1_ref, w3_ref, w2_ref, o_ref):
    ti = pl.program_id(1)
    nt = pl.num_programs(1)
    tm = x1_ref.shape[1]
    has_prev = (ti > 0).astype(F32)
    has_next = (ti < nt - 1).astype(F32)

    y = y0_ref[0] + y1_ref[0]
    inv_n = 1.0 / HEAD_DIM
    ones = ones_ref[...]
    mean = _dot(y.astype(BF16), ones) * inv_n
    dlt = y - mean
    var = _dot((dlt * dlt).astype(BF16), ones) * inv_n
    yn = dlt * lax.rsqrt(var + GN_EPS) * lng_ref[...] + lnb_ref[...]
    zg = _token_shift(zg_ref[0], zgp_ref[0] * has_prev, zgn_ref[0] * has_next, mug_ref[...])
    gate = _dot(jax.nn.sigmoid(zg).astype(BF16), g2_ref[...])
    m_r = ((yn + bonus_ref[0]) * gate).astype(BF16)

    ext = jnp.concatenate([zqp_ref[0] * has_prev, zq_ref[0], zqn_ref[0] * has_next], axis=0)
    n = ext.shape[0]
    tglob = ti * tm + lax.broadcasted_iota(jnp.int32, (tm, POOL_GROUP), 0)
    t_total = nt * tm
    cur = ext
    pooled = []
    for gi, win in enumerate(POOL_WINDOWS):
        if gi == 0:
            cur = cur + pltpu.roll(cur, 1, 0)
        else:
            sh = win // 4
            cur = cur[:, POOL_GROUP:]
            cur = pltpu.roll(cur, n - sh, 0) + pltpu.roll(cur, sh, 0)
        wsum = cur[HALO:HALO + tm, 0:POOL_GROUP]
        half = win // 2
        cnt = (jnp.minimum(tglob + half, t_total) - jnp.maximum(tglob - half, 0)).astype(F32)
        zc = ext[HALO:HALO + tm, gi * POOL_GROUP:(gi + 1) * POOL_GROUP]
        pg = (wsum / cnt - zc).astype(BF16)
        po = _dot(pg, pw_ref[gi]) * ps_ref[:, gi * POOL_GROUP:(gi + 1) * POOL_GROUP]
        pooled.append(po.astype(BF16))

    m_all = jnp.concatenate(pooled + [m_r], axis=1)
    m_out = _dot(m_all, wout_ref[...])
    x2 = x1_ref[0] + mod_ref[0, 5:6, :] * _rms(m_out, nmpost_ref[...])
    o_ref[0] = _ffn(x2, mod_ref, 6, pre_ref, post_ref, w1_ref, w3_ref, w2_ref)


def _mix_out_call(x1, mod, y0, y1, bonus, z, p):
    nb, t, d = x1.shape
    f = p["f2_w1"].shape[1]
    tm = ROW_TILE
    gw = 2 * LANE
    row = lambda b, i: (b, i, 0)
    g_main, g_prev, g_next = _halo_maps(tm, t, Z_RWKV // gw - 1)
    q_main, q_prev, q_next = _halo_maps(tm, t, Z_RWKV // D_POOL)
    return pl.pallas_call(
        _mix_out_kernel,
        grid=(nb, t // tm),
        in_specs=[
            pl.BlockSpec((1, tm, d), row),
            pl.BlockSpec((1, N_MOD, d), lambda b, i: (b, 0, 0)),
            pl.BlockSpec((1, tm, D_RWKV), row),
            pl.BlockSpec((1, tm, D_RWKV), row),
            pl.BlockSpec((1, tm, D_RWKV), row),
            pl.BlockSpec((1, tm, gw), g_main),
            pl.BlockSpec((1, HALO, gw), g_prev),
            pl.BlockSpec((1, HALO, gw), g_next),
            pl.BlockSpec((1, tm, D_POOL), q_main),
            pl.BlockSpec((1, HALO, D_POOL), q_prev),
            pl.BlockSpec((1, HALO, D_POOL), q_next),
            _const_spec((1, gw)), _const_spec((D_RWKV, D_RWKV)), _const_spec((gw, D_RWKV)),
            _const_spec((1, D_RWKV)), _const_spec((1, D_RWKV)),
            _const_spec((len(POOL_WINDOWS), POOL_GROUP, POOL_GROUP)), _const_spec((1, D_POOL)),
            _const_spec((d, d)), _const_spec((1, d)),
            _const_spec((1, d)), _const_spec((1, d)),
            _const_spec((d, f)), _const_spec((d, f)), _const_spec((f, d)),
        ],
        out_specs=pl.BlockSpec((1, tm, d), row),
        out_shape=jax.ShapeDtypeStruct((nb, t, d), F32),
        compiler_params=pltpu.CompilerParams(
            dimension_semantics=("parallel", "parallel"), vmem_limit_bytes=VMEM_LIMIT),
        name="mix_out",
    )(x1, mod, y0, y1, bonus, z, z, z, z, z, z, p["mu_g"], p["ones"], p["g2p"],
      p["lnx_g"], p["lnx_b"], p["pool_w"], p["pool_scale"], p["w_out"], p["nm_post"],
      p["n2_pre"], p["n2_post"], p["f2_w1"], p["f2_w3"], p["f2_w2"])


def _tri_consts(tb):
    i = np.arange(tb)[:, None]
    j = np.arange(tb)[None, :]
    same = (i // CHUNK) == (j // CHUNK)
    return jnp.asarray(same & (j <= i), BF16), jnp.asarray(same & (j >= i), BF16)


def _head_ones():
    i = np.arange(D_RWKV)
    return jnp.asarray((i[:, None] // HEAD_DIM) == (i[None, :] // HEAD_DIM), BF16)


def _layer(x, c_mod, p):
    x1, z = _ffn_in_call(x, c_mod, p["n1_pre"], p["n1_post"], p["f1_w1"], p["f1_w3"], p["f1_w2"],
                         p["nm_pre"], p["w_in"])
    v16, bonus, ops = _prep_call(z, p)
    y0 = _scan_call(False, v16, ops[0])
    y1 = _scan_call(True, v16, ops[1])
    return _mix_out_call(x1, c_mod, y0, y1, bonus, z, p)


def _prepare(i, n1_pre, n1_post, f1_w1, f1_w3, f1_w2, nm_pre, nm_post, w_in, shift_mu, pool_w,
             pool_scale, w0, w2, a0, a2, g2, k_k, k_a, r_k, lnx_g, lnx_b, w_out,
             n2_pre, n2_post, f2_w1, f2_w3, f2_w2):
    row = lambda a: a[i].reshape(1, -1)
    n_real = w_in.shape[2] - D_POOL
    pad = Z_RWKV - n_real
    w_in_i = w_in[i]
    win = jnp.concatenate(
        [w_in_i[:, D_POOL:], jnp.zeros((D_MODEL, pad), F32), w_in_i[:, :D_POOL]], axis=1)
    mu_r = jnp.concatenate([shift_mu[i], jnp.zeros((pad,), F32)]).reshape(1, Z_RWKV)
    half = LANE // 2
    zpad = jnp.zeros((half, D_RWKV), F32)
    w2p = jnp.concatenate([jnp.concatenate([w2[i, 0], zpad], 0),
                           jnp.concatenate([zpad, w2[i, 1]], 0)], axis=1)
    a2p = jnp.concatenate([jnp.concatenate([a2[i, 0], zpad], 0),
                           jnp.concatenate([zpad, a2[i, 1]], 0)], axis=1)
    g2p = jnp.concatenate([g2[i], jnp.zeros((2 * LANE - R_GATE, D_RWKV), F32)], 0)
    tri_f, tri_b = _tri_consts(ROW_TILE)
    return dict(
        n1_pre=row(n1_pre), n1_post=row(n1_post),
        f1_w1=f1_w1[i].astype(BF16), f1_w3=f1_w3[i].astype(BF16), f1_w2=f1_w2[i].astype(BF16),
        nm_pre=row(nm_pre), nm_post=row(nm_post), w_in=win.astype(BF16),
        mu_r=mu_r, mu_g=mu_r[:, Z_RWKV - 2 * LANE:],
        pool_w=pool_w[i].astype(BF16), pool_scale=row(pool_scale),
        w0=w0[i].reshape(1, -1), w2p=w2p.astype(BF16),
        a0=a0[i].reshape(1, -1), a2p=a2p.astype(BF16),
        g2p=g2p.astype(BF16), k_k=row(k_k), k_a=row(k_a), r_k=row(r_k),
        lnx_g=row(lnx_g), lnx_b=row(lnx_b), w_out=w_out[i].astype(BF16),
        n2_pre=row(n2_pre), n2_post=row(n2_post),
        f2_w1=f2_w1[i].astype(BF16), f2_w3=f2_w3[i].astype(BF16), f2_w2=f2_w2[i].astype(BF16),
        tri_f=tri_f, tri_b=tri_b, ones=_head_ones(),
    )


def kernel(x_prompt, x_sample, c_prompt, c_sample, ada_w, ada_b, n1_pre, n1_post, f1_w1, f1_w3, f1_w2, nm_pre, nm_post, w_in, shift_mu, pool_w, pool_scale, w0, w2, a0, a2, g2, k_k, k_a, r_k, lnx_g, lnx_b, w_out, n2_pre, n2_post, f2_w1, f2_w3, f2_w2):
    depth = ada_w.shape[0]
    nbp = x_prompt.shape[0]
    xs = [x_prompt, x_sample]
    c_all = jnp.concatenate([c_prompt, c_sample], axis=0)
    for i in range(depth):
        p = _prepare(i, n1_pre, n1_post, f1_w1, f1_w3, f1_w2, nm_pre, nm_post, w_in, shift_mu,
                     pool_w, pool_scale, w0, w2, a0, a2, g2, k_k, k_a, r_k, lnx_g, lnx_b, w_out,
                     n2_pre, n2_post, f2_w1, f2_w3, f2_w2)
        mod = _mod_call(c_all, ada_w[i], ada_b[i]).reshape(c_all.shape[0], N_MOD, D_MODEL)
        mods = [mod[:nbp], mod[nbp:]]
        xs = [_layer(x, m, p) for x, m in zip(xs, mods)]
    return (xs[0], xs[1])
```
